```python
import math
import jax
import jax.numpy as jnp
from jax import lax
import numpy as np

D_MODEL = 1024
BATCH = 8
SEQ = 2048
DEPTH = 2
DEC_BATCH = 128
DEC_SEQ = 8
PAST_LEN = 2048
PAGE_SIZE = 128

N_HEADS = 4
HEAD_DIM = 64
BR_W = N_HEADS * HEAD_DIM
N_BRANCH = 5
S5_GROUP = 16
S5_GROUPS = BR_W // S5_GROUP
S5_STATE = 64
MOBA_BLOCK = 256
MOBA_TOPK = 3
MOBA_QBLK = 32
ATTN_QBLK = 128
DIFF_DK = HEAD_DIM // 2
GDN_CONV = 4
GDN_CHUNK = 64
ROPE_THETA = 500000.0
ROT_FRAC = 4
MEM_LEN = 256
MEM_HEADS = 4
EPS = 1e-6
NEG = -1e30
IN_SIZES = (BR_W, BR_W, BR_W, BR_W, BR_W, BR_W, BR_W, BR_W, BR_W, BR_W, 3 * BR_W, N_HEADS, N_HEADS, BR_W, MEM_HEADS * HEAD_DIM, N_BRANCH * D_MODEL)
D_IN = sum(IN_SIZES)

kernel_name = 'hybrid_s5_moba_diff_gdn_decoder_step'


def rms_norm(x, g):
    xf = x.astype(jnp.float32)
    y = xf * lax.rsqrt(jnp.mean(xf * xf, axis=-1, keepdims=True) + EPS)
    return (y * g.astype(jnp.float32)).astype(x.dtype)


def l2_normalize(x):
    xf = x.astype(jnp.float32)
    return xf * lax.rsqrt(jnp.sum(xf * xf, axis=-1, keepdims=True) + EPS)


def partial_rope(x, pos):
    d = x.shape[-1]
    r = d // ROT_FRAC
    half = r // 2
    inv = ROPE_THETA ** (-jnp.arange(half, dtype=jnp.float32) / half)
    ang = pos.astype(jnp.float32)[:, None] * inv[None, :]
    cos = jnp.cos(ang)[None, :, None, :]
    sin = jnp.sin(ang)[None, :, None, :]
    xf = x.astype(jnp.float32)
    x1 = xf[..., :half]
    x2 = xf[..., half:r]
    out = jnp.concatenate([x1 * cos - x2 * sin, x2 * cos + x1 * sin, xf[..., r:]], axis=-1)
    return out.astype(x.dtype)


def split_proj(proj):
    outs = []
    o = 0
    for s in IN_SIZES:
        outs.append(proj[..., o:o + s])
        o += s
    return outs


def gather_pages(pool, page_table):
    g = pool[page_table]
    b, n, p = g.shape[:3]
    return g.reshape((b, n * p) + g.shape[3:])


def s5_scan(u, lp, h0_re, h0_im):
    b, t, _ = u.shape
    f32 = jnp.float32
    uf = u.astype(f32).reshape(b, t, S5_GROUPS, S5_GROUP)
    lr = lp['s5_lam_re'].astype(f32)
    li = lp['s5_lam_im'].astype(f32)
    dt = jnp.exp(lp['s5_log_dt'].astype(f32))[:, None]
    mag = jnp.exp(lr * dt)
    ab_re = mag * jnp.cos(li * dt)
    ab_im = mag * jnp.sin(li * dt)
    den = lr * lr + li * li
    nr = ab_re - 1.0
    f_re = (nr * lr + ab_im * li) / den
    f_im = (ab_im * lr - nr * li) / den
    br = lp['s5_b_re'].astype(f32)
    bim = lp['s5_b_im'].astype(f32)
    bb_re = f_re[..., None] * br - f_im[..., None] * bim
    bb_im = f_re[..., None] * bim + f_im[..., None] * br
    bu_re = jnp.einsum('btgc,gpc->btgp', uf, bb_re)
    bu_im = jnp.einsum('btgc,gpc->btgp', uf, bb_im)
    h0r = h0_re.astype(f32)
    h0i = h0_im.astype(f32)
    bu_re = bu_re.at[:, 0].add(ab_re * h0r - ab_im * h0i)
    bu_im = bu_im.at[:, 0].add(ab_re * h0i + ab_im * h0r)
    a_re = jnp.broadcast_to(ab_re, bu_re.shape)
    a_im = jnp.broadcast_to(ab_im, bu_im.shape)

    def combine(e1, e2):
        a1r, a1i, b1r, b1i = e1
        a2r, a2i, b2r, b2i = e2
        return (a2r * a1r - a2i * a1i, a2r * a1i + a2i * a1r,
                a2r * b1r - a2i * b1i + b2r, a2r * b1i + a2i * b1r + b2i)

    _, _, h_re, h_im = lax.associative_scan(combine, (a_re, a_im, bu_re, bu_im), axis=1)
    y = (jnp.einsum('btgp,gcp->btgc', h_re, lp['s5_c_re'].astype(f32))
         - jnp.einsum('btgp,gcp->btgc', h_im, lp['s5_c_im'].astype(f32))
         + lp['s5_d'].astype(f32) * uf)
    return y.reshape(b, t, BR_W).astype(u.dtype), h_re[:, -1], h_im[:, -1]


def moba_attention(q, k, v, q_pos0):
    b, tq, h, d = q.shape
    length = k.shape[1]
    nb = -(-length // MOBA_BLOCK)
    pad = ((0, 0), (0, nb * MOBA_BLOCK - length), (0, 0), (0, 0))
    kb = jnp.pad(k, pad).reshape(b, nb, MOBA_BLOCK, h, d).transpose(0, 3, 1, 2, 4)
    vb = jnp.pad(v, pad).reshape(b, nb, MOBA_BLOCK, h, d).transpose(0, 3, 1, 2, 4)
    kmean = jnp.mean(kb.astype(jnp.float32), axis=3)
    n_sel = min(MOBA_TOPK, nb - 1)
    qb = min(MOBA_QBLK, tq)
    nq = -(-tq // qb)
    qp = jnp.pad(q, ((0, 0), (0, nq * qb - tq), (0, 0), (0, 0)))
    qc = qp.reshape(b, nq, qb, h, d).transpose(1, 0, 3, 2, 4)
    bi = jnp.arange(b)[:, None, None, None]
    hi = jnp.arange(h)[None, :, None, None]
    blk = jnp.arange(nb)
    offs = jnp.arange(MOBA_BLOCK)
    scale = d ** -0.5

    def one(args):
        qi, ci = args
        qpos = q_pos0 + ci * qb + jnp.arange(qb)
        own = qpos // MOBA_BLOCK
        own_idx = jnp.broadcast_to(own[None, None, :, None], (b, h, qb, 1))
        if n_sel > 0:
            gs = jnp.einsum('bhqd,bhnd->bhqn', qi.astype(jnp.float32), kmean)
            past = blk[None, :] < own[:, None]
            gs = jnp.where(past, gs, -jnp.inf)
            top_s, top_i = lax.top_k(gs, n_sel)
            idx = jnp.concatenate([top_i, own_idx], axis=-1)
            ok = jnp.concatenate([jnp.isfinite(top_s), jnp.ones((b, h, qb, 1), dtype=bool)], axis=-1)
        else:
            idx = own_idx
            ok = jnp.ones((b, h, qb, 1), dtype=bool)
        kg = kb[bi, hi, idx]
        vg = vb[bi, hi, idx]
        kpos = idx[..., None] * MOBA_BLOCK + offs
        mask = ok[..., None] & (kpos <= qpos[None, None, :, None, None])
        s = jnp.einsum('bhqd,bhqnsd->bhqns', qi, kg, preferred_element_type=jnp.float32) * scale
        s = jnp.where(mask, s, NEG)
        ns = s.shape[3]
        p = jax.nn.softmax(s.reshape(b, h, qb, ns * MOBA_BLOCK), axis=-1).reshape(s.shape)
        return jnp.einsum('bhqns,bhqnsd->bhqd', p.astype(v.dtype), vg)

    o = lax.map(one, (qc, jnp.arange(nq)))
    return o.transpose(1, 0, 3, 2, 4).reshape(b, nq * qb, h, d)[:, :tq]


def diff_attention(q, k, v, q_pos0, lam):
    b, tq, h = q.shape[:3]
    length = k.shape[1]
    qb = min(ATTN_QBLK, tq)
    nq = -(-tq // qb)
    qp = jnp.pad(q, ((0, 0), (0, nq * qb - tq), (0, 0), (0, 0), (0, 0)))
    qc = jnp.moveaxis(qp.reshape((b, nq, qb) + q.shape[2:]), 1, 0)
    kpos = jnp.arange(length)
    scale = DIFF_DK ** -0.5

    def one(args):
        qi, ci = args
        qpos = q_pos0 + ci * qb + jnp.arange(qb)
        s = jnp.einsum('bqhcd,bkhcd->bhcqk', qi, k, preferred_element_type=jnp.float32) * scale
        s = jnp.where(kpos[None, :] <= qpos[:, None], s, NEG)
        p = jax.nn.softmax(s, axis=-1)
        wgt = p[:, :, 0] - lam * p[:, :, 1]
        return jnp.einsum('bhqk,bkhd->bqhd', wgt.astype(v.dtype), v)

    o = lax.map(one, (qc, jnp.arange(nq)))
    return jnp.moveaxis(o, 0, 1).reshape(b, nq * qb, h, v.shape[-1])[:, :tq]


def causal_conv(x, prev, w):
    t = x.shape[1]
    xx = jnp.concatenate([prev.astype(x.dtype), x], axis=1)
    out = w[0] * xx[:, 0:t]
    for j in range(1, GDN_CONV):
        out = out + w[j] * xx[:, j:j + t]
    return jax.nn.silu(out), xx[:, t:]


def gated_delta(q, k, v, beta, g, s0):
    b, t, h, dk = q.shape
    dv = v.shape[-1]
    c = min(GDN_CHUNK, t)
    n = -(-t // c)
    tp = n * c

    def chunks(a):
        a = a.astype(jnp.float32)
        a = jnp.pad(a, ((0, 0), (0, tp - t)) + ((0, 0),) * (a.ndim - 2))
        a = a.reshape((b, n, c) + a.shape[2:])
        return jnp.moveaxis(jnp.moveaxis(a, 1, 0), 3, 2)

    qc, kc, vc, bc = chunks(q), chunks(k), chunks(v), chunks(beta)
    gcum = jnp.cumsum(chunks(g), axis=-1)
    tril = jnp.tril(jnp.ones((c, c), dtype=bool))
    strict = jnp.tril(jnp.ones((c, c), dtype=bool), -1)
    gdiff = gcum[..., :, None] - gcum[..., None, :]
    decay = jnp.where(tril, jnp.exp(jnp.where(tril, gdiff, 0.0)), 0.0)
    kbeta = kc * bc[..., None]
    lmat = jnp.where(strict, jnp.einsum('nbhik,nbhjk->nbhij', kbeta, kc) * decay, 0.0)
    amat = lmat + jnp.eye(c, dtype=jnp.float32)
    u = lax.linalg.triangular_solve(amat, vc * bc[..., None], left_side=True, lower=True, unit_diagonal=True)
    w = lax.linalg.triangular_solve(amat, kbeta * jnp.exp(gcum)[..., None], left_side=True, lower=True, unit_diagonal=True)
    qk = jnp.where(tril, jnp.einsum('nbhik,nbhjk->nbhij', qc, kc) * decay, 0.0)

    def step(s, inp):
        qi, ki, ui, wi, gi, qki = inp
        vnew = ui - jnp.einsum('bhck,bhkv->bhcv', wi, s)
        o = (jnp.einsum('bhck,bhkv->bhcv', qi * jnp.exp(gi)[..., None], s)
             + jnp.einsum('bhij,bhjv->bhiv', qki, vnew))
        glast = gi[..., -1:]
        s = (s * jnp.exp(glast)[..., None]
             + jnp.einsum('bhck,bhcv->bhkv', ki * jnp.exp(glast - gi)[..., None], vnew))
        return s, o

    s_fin, o = lax.scan(step, s0.astype(jnp.float32), (qc, kc, u, w, gcum, qk))
    o = jnp.moveaxis(o, 0, 1).transpose(0, 1, 3, 2, 4).reshape(b, tp, h, dv)[:, :t]
    return o, s_fin


def memory_kv(mem, g, w):
    b, m, _ = mem.shape
    kv = rms_norm(mem, g) @ w
    k, v = jnp.split(kv, 2, axis=-1)
    return k.reshape(b, m, MEM_HEADS, HEAD_DIM), v.reshape(b, m, MEM_HEADS, HEAD_DIM)


def memory_attention(q, mk, mv):
    s = jnp.einsum('bthd,bmhd->bhtm', q, mk, preferred_element_type=jnp.float32) * HEAD_DIM ** -0.5
    p = jax.nn.softmax(s, axis=-1)
    return jnp.einsum('bhtm,bmhd->bthd', p.astype(mv.dtype), mv)


def mixer_layer(x, pos0, lidx, lp, mem_k, mem_v, s5_re, s5_im, conv_prev, gdn_s, past_mk, past_mv, past_dk, past_dv):
    b, t, _ = x.shape
    f32 = jnp.float32
    pos = pos0 + jnp.arange(t)
    hn = rms_norm(x, lp['norm_pre'])
    proj = hn @ lp['w_in']
    (a_u, a_z, b_q, b_k, b_v, b_z, c_q, c_k, c_v, c_z,
     d_qkv, d_beta, d_a, d_z, x_q, gates) = split_proj(proj)

    ya, s5_re_new, s5_im_new = s5_scan(a_u, lp, s5_re, s5_im)
    glu = jax.nn.gelu(ya) @ lp['w_glu']
    ya = glu[..., :BR_W] * jax.nn.sigmoid(glu[..., BR_W:])
    ya = ya * jax.nn.silu(a_z)

    mq = partial_rope(b_q.reshape(b, t, N_HEADS, HEAD_DIM), pos)
    mk = partial_rope(b_k.reshape(b, t, N_HEADS, HEAD_DIM), pos)
    mv = b_v.reshape(b, t, N_HEADS, HEAD_DIM)
    mk_all = mk if past_mk is None else jnp.concatenate([past_mk.astype(mk.dtype), mk], axis=1)
    mv_all = mv if past_mv is None else jnp.concatenate([past_mv.astype(mv.dtype), mv], axis=1)
    yb = moba_attention(mq, mk_all, mv_all, pos0).reshape(b, t, BR_W) * jax.nn.silu(b_z)

    dq = partial_rope(c_q.reshape(b, t, 2 * N_HEADS, DIFF_DK), pos).reshape(b, t, N_HEADS, 2, DIFF_DK)
    dk = partial_rope(c_k.reshape(b, t, 2 * N_HEADS, DIFF_DK), pos).reshape(b, t, N_HEADS, HEAD_DIM)
    dv = c_v.reshape(b, t, N_HEADS, HEAD_DIM)
    dk_all = dk if past_dk is None else jnp.concatenate([past_dk.astype(dk.dtype), dk], axis=1)
    dv_all = dv if past_dv is None else jnp.concatenate([past_dv.astype(dv.dtype), dv], axis=1)
    lam_init = 0.8 - 0.6 * math.exp(-0.3 * lidx)
    lv = lp['diff_lam'].astype(f32)
    lam = jnp.exp(jnp.sum(lv[0] * lv[1])) - jnp.exp(jnp.sum(lv[2] * lv[3])) + lam_init
    oc = diff_attention(dq, dk_all.reshape(b, -1, N_HEADS, 2, DIFF_DK), dv_all, pos0, lam)
    yc = (rms_norm(oc, lp['diff_norm']) * (1.0 - lam_init)).reshape(b, t, BR_W) * jax.nn.silu(c_z)

    conv_out, conv_new = causal_conv(d_qkv, conv_prev, lp['gdn_conv_w'])
    gq, gk, gv = jnp.split(conv_out, 3, axis=-1)
    gq = l2_normalize(gq.reshape(b, t, N_HEADS, HEAD_DIM)) * HEAD_DIM ** -0.5
    gk = l2_normalize(gk.reshape(b, t, N_HEADS, HEAD_DIM))
    gv = gv.reshape(b, t, N_HEADS, HEAD_DIM)
    beta = jax.nn.sigmoid(d_beta.astype(f32))
    gdec = -jnp.exp(lp['gdn_a_log'].astype(f32)) * jax.nn.softplus(d_a.astype(f32) + lp['gdn_dt_bias'].astype(f32))
    od, s_new = gated_delta(gq, gk, gv, beta, gdec, gdn_s)
    yd = rms_norm(od, lp['gdn_norm']).astype(x.dtype).reshape(b, t, BR_W) * jax.nn.silu(d_z)

    yx = memory_attention(x_q.reshape(b, t, MEM_HEADS, HEAD_DIM), mem_k.astype(x.dtype), mem_v.astype(x.dtype)).reshape(b, t, BR_W)

    ys = jnp.stack([ya.astype(x.dtype), yb, yc, yd, yx], axis=2)
    zb = jnp.einsum('btnw,nwd->btnd', ys, lp['w_branch'])
    gt = jax.nn.sigmoid(gates.reshape(b, t, N_BRANCH, D_MODEL))
    merged = jnp.sum(gt * zb, axis=2)
    out = merged @ lp['w_out']
    x_new = x + rms_norm(out, lp['norm_post'])
    return x_new, (s5_re_new, s5_im_new, mk, mv, dk, dv, conv_new, s_new)


def setup_inputs(seed: int = 0) -> dict:
    key = jax.random.key(seed)
    keys = list(jax.random.split(key, 48))

    def nrm(shape, s=1.0):
        return s * jax.random.normal(keys.pop(), shape, jnp.float32)

    def unif(shape, lo, hi):
        return jax.random.uniform(keys.pop(), shape, jnp.float32, minval=lo, maxval=hi)

    n_pages = PAST_LEN // PAGE_SIZE
    n_phys = (DEC_BATCH * n_pages * 5 + 3) // 4
    perm = jax.random.permutation(keys.pop(), n_phys)
    page_table = perm[:DEC_BATCH * n_pages].reshape(DEC_BATCH, n_pages).astype(jnp.int32)
    kv_shape = (DEPTH, n_phys, PAGE_SIZE, N_HEADS, HEAD_DIM)
    dt = jnp.exp(unif((DEPTH, N_HEADS), math.log(1e-3), math.log(1e-1)))
    return {
        'x_prompt': nrm((BATCH, SEQ, D_MODEL)),
        'x_sample': nrm((DEC_BATCH, DEC_SEQ, D_MODEL)),
        'state_s5_re': nrm((DEPTH, DEC_BATCH, S5_GROUPS, S5_STATE), 0.3),
        'state_s5_im': nrm((DEPTH, DEC_BATCH, S5_GROUPS, S5_STATE), 0.3),
        'cache_moba_k': nrm(kv_shape),
        'cache_moba_v': nrm(kv_shape),
        'cache_diff_k': nrm(kv_shape),
        'cache_diff_v': nrm(kv_shape),
        'state_gdn_conv': nrm((DEPTH, DEC_BATCH, GDN_CONV - 1, 3 * BR_W)),
        'state_gdn': nrm((DEPTH, DEC_BATCH, N_HEADS, HEAD_DIM, HEAD_DIM), 0.1),
        'cache_mem_k': nrm((DEPTH, DEC_BATCH, MEM_LEN, MEM_HEADS, HEAD_DIM)),
        'cache_mem_v': nrm((DEPTH, DEC_BATCH, MEM_LEN, MEM_HEADS, HEAD_DIM)),
        'page_table': page_table,
        'mem_prompt': nrm((BATCH, MEM_LEN, D_MODEL)),
        'norm_pre': 1.0 + nrm((DEPTH, D_MODEL), 0.02),
        'norm_post': 1.0 + nrm((DEPTH, D_MODEL), 0.02),
        'norm_mem': 1.0 + nrm((DEPTH, D_MODEL), 0.02),
        'w_in': nrm((DEPTH, D_MODEL, D_IN), D_MODEL ** -0.5),
        's5_lam_re': -0.5 + nrm((DEPTH, S5_GROUPS, S5_STATE), 0.01),
        's5_lam_im': math.pi * jnp.arange(S5_STATE, dtype=jnp.float32) + nrm((DEPTH, S5_GROUPS, S5_STATE), 0.01),
        's5_log_dt': unif((DEPTH, S5_GROUPS), math.log(1e-3), math.log(1e-1)),
        's5_b_re': nrm((DEPTH, S5_GROUPS, S5_STATE, S5_GROUP), (2 * S5_GROUP) ** -0.5),
        's5_b_im': nrm((DEPTH, S5_GROUPS, S5_STATE, S5_GROUP), (2 * S5_GROUP) ** -0.5),
        's5_c_re': nrm((DEPTH, S5_GROUPS, S5_GROUP, S5_STATE), S5_STATE ** -0.5),
        's5_c_im': nrm((DEPTH, S5_GROUPS, S5_GROUP, S5_STATE), S5_STATE ** -0.5),
        's5_d': nrm((DEPTH, S5_GROUPS, S5_GROUP)),
        'w_glu': nrm((DEPTH, BR_W, 2 * BR_W), BR_W ** -0.5),
        'diff_lam': nrm((DEPTH, 4, DIFF_DK), 0.1),
        'diff_norm': 1.0 + nrm((DEPTH, HEAD_DIM), 0.02),
        'gdn_conv_w': nrm((DEPTH, GDN_CONV, 3 * BR_W), GDN_CONV ** -0.5),
        'gdn_a_log': jnp.log(unif((DEPTH, N_HEADS), 1.0, 16.0)),
        'gdn_dt_bias': dt + jnp.log(-jnp.expm1(-dt)),
        'gdn_norm': 1.0 + nrm((DEPTH, HEAD_DIM), 0.02),
        'w_mem_kv': nrm((DEPTH, D_MODEL, 2 * MEM_HEADS * HEAD_DIM), D_MODEL ** -0.5),
        'w_branch': nrm((DEPTH, N_BRANCH, BR_W, D_MODEL), BR_W ** -0.5),
        'w_out': nrm((DEPTH, D_MODEL, D_MODEL), D_MODEL ** -0.5),
    }


def reference(x_prompt, x_sample, state_s5_re, state_s5_im, cache_moba_k, cache_moba_v, cache_diff_k, cache_diff_v, state_gdn_conv, state_gdn, cache_mem_k, cache_mem_v, page_table, mem_prompt, norm_pre, norm_post, norm_mem, w_in, s5_lam_re, s5_lam_im, s5_log_dt, s5_b_re, s5_b_im, s5_c_re, s5_c_im, s5_d, w_glu, diff_lam, diff_norm, gdn_conv_w, gdn_a_log, gdn_dt_bias, gdn_norm, w_mem_kv, w_branch, w_out):
    xp = x_prompt
    xs = x_sample
    bp = xp.shape[0]
    zero_s5 = jnp.zeros((bp, S5_GROUPS, S5_STATE), jnp.float32)
    zero_conv = jnp.zeros((bp, GDN_CONV - 1, 3 * BR_W), xp.dtype)
    zero_gdn = jnp.zeros((bp, N_HEADS, HEAD_DIM, HEAD_DIM), jnp.float32)
    p_st = [[] for _ in range(8)]
    s_st = [[] for _ in range(8)]
    p_mk, p_mv = [], []
    for l in range(DEPTH):
        lp = {
            'norm_pre': norm_pre[l], 'norm_post': norm_post[l], 'w_in': w_in[l],
            's5_lam_re': s5_lam_re[l], 's5_lam_im': s5_lam_im[l], 's5_log_dt': s5_log_dt[l],
            's5_b_re': s5_b_re[l], 's5_b_im': s5_b_im[l], 's5_c_re': s5_c_re[l], 's5_c_im': s5_c_im[l],
            's5_d': s5_d[l], 'w_glu': w_glu[l], 'diff_lam': diff_lam[l], 'diff_norm': diff_norm[l],
            'gdn_conv_w': gdn_conv_w[l], 'gdn_a_log': gdn_a_log[l], 'gdn_dt_bias': gdn_dt_bias[l],
            'gdn_norm': gdn_norm[l], 'w_branch': w_branch[l], 'w_out': w_out[l],
        }
        mk, mv = memory_kv(mem_prompt, norm_mem[l], w_mem_kv[l])
        p_mk.append(mk)
        p_mv.append(mv)
        xp, st = mixer_layer(xp, 0, l, lp, mk, mv, zero_s5, zero_s5, zero_conv, zero_gdn, None, None, None, None)
        for lst, a in zip(p_st, st):
            lst.append(a)
        xs, st = mixer_layer(xs, PAST_LEN, l, lp, cache_mem_k[l], cache_mem_v[l],
                             state_s5_re[l], state_s5_im[l], state_gdn_conv[l], state_gdn[l],
                             gather_pages(cache_moba_k[l], page_table), gather_pages(cache_moba_v[l], page_table),
                             gather_pages(cache_diff_k[l], page_table), gather_pages(cache_diff_v[l], page_table))
        for lst, a in zip(s_st, st):
            lst.append(a)
    y_prompt = xp
    y_sample = xs
    p_s5_re, p_s5_im, p_moba_k, p_moba_v, p_diff_k, p_diff_v, p_gdn_conv, p_gdn = [jnp.stack(a) for a in p_st]
    s_s5_re, s_s5_im, s_moba_k, s_moba_v, s_diff_k, s_diff_v, s_gdn_conv, s_gdn = [jnp.stack(a) for a in s_st]
    p_mem_k = jnp.stack(p_mk)
    p_mem_v = jnp.stack(p_mv)
    return (y_prompt, y_sample, p_s5_re, p_s5_im, p_moba_k, p_moba_v, p_diff_k, p_diff_v, p_gdn_conv, p_gdn, p_mem_k, p_mem_v, s_s5_re, s_s5_im, s_moba_k, s_moba_v, s_diff_k, s_diff_v, s_gdn_conv, s_gdn)
```

```python
import functools
import math

import jax
import jax.numpy as jnp
from jax import lax
from jax.experimental import pallas as pl
from jax.experimental.pallas import tpu as pltpu

F32 = jnp.float32
BF16 = jnp.bfloat16
HI = lax.Precision.HIGHEST

D_MODEL = 1024
N_HEADS = 4
HEAD_DIM = 64
BR_W = N_HEADS * HEAD_DIM
N_BRANCH = 5
S5_GROUP = 16
S5_GROUPS = 16
S5_STATE = 64
S5_W = S5_GROUPS * S5_STATE
MOBA_BLOCK = 256
MOBA_TOPK = 3
DIFF_DK = 32
GDN_CONV = 4
GDN_CHUNK = 64
ROPE_THETA = 500000.0
ROT_FRAC = 4
MEM_LEN = 256
PAGE_SIZE = 128
EPS = 1e-6
NEG = -1e30

D_PAD = 9216
NCB = D_PAD // BR_W
N_GATE_CB = N_BRANCH * D_MODEL // BR_W
(CB_AU, CB_AZ, CB_BQ, CB_BK, CB_BV, CB_BZ, CB_CQ, CB_CK, CB_CV, CB_CZ,
 CB_DQ, CB_DK, CB_DV, CB_DZ, CB_XQ) = range(N_GATE_CB, N_GATE_CB + 15)
COL_SMALL = (N_GATE_CB + 15) * BR_W
CB128_SMALL = COL_SMALL // 128

VMEM_LIMIT = 56 * 1024 * 1024


def _cparams(sem):
    return pltpu.CompilerParams(dimension_semantics=sem, vmem_limit_bytes=VMEM_LIMIT)


def _dot(a, b, precision=None):
    return jnp.dot(a, b, preferred_element_type=F32, precision=precision)


def _dot_nt(a, b, precision=None):
    return lax.dot_general(a, b, (((1,), (1,)), ((), ())), preferred_element_type=F32, precision=precision)


def _dot_tn(a, b, precision=None):
    return lax.dot_general(a, b, (((0,), (0,)), ((), ())), preferred_element_type=F32, precision=precision)


def _head_masks(width=BR_W, group=HEAD_DIM):
    lane = lax.broadcasted_iota(jnp.int32, (1, width), 1)
    return [(lane // group) == h for h in range(width // group)]


def _sigmoid(x):
    return 1.0 / (1.0 + jnp.exp(-x))


def _silu(x):
    return x * _sigmoid(x)


def _proj_kernel(x_ref, g_ref, w_ref, p_ref, *rest, with_tm, tn):
    if with_tm:
        au_ref, hn_ref = rest
    else:
        (hn_ref,) = rest
    j = pl.program_id(1)

    @pl.when(j == 0)
    def _():
        x = x_ref[...]
        y = x * lax.rsqrt(jnp.mean(x * x, axis=-1, keepdims=True) + EPS) * g_ref[...]
        hn_ref[...] = y.astype(BF16)

    p = _dot(hn_ref[...], w_ref[...])
    p_ref[...] = p
    if with_tm:
        col = CB_AU * BR_W
        @pl.when(j == col // tn)
        def _():
            au_ref[...] = p[:, col % tn:col % tn + BR_W]


def _proj(x2d, g, w, *, tm, tn, seq_len=None):
    n, d = x2d.shape
    dout = w.shape[1]
    with_tm = seq_len is not None
    out_shape = [jax.ShapeDtypeStruct((n, dout), F32)]
    out_specs = [pl.BlockSpec((tm, tn), lambda i, j: (i, j))]
    if with_tm:
        nb = n // seq_len
        tpb = seq_len // tm
        out_shape.append(jax.ShapeDtypeStruct((seq_len, nb * BR_W), F32))
        out_specs.append(pl.BlockSpec((tm, BR_W), lambda i, j: (i % tpb, i // tpb)))
    res = pl.pallas_call(
        functools.partial(_proj_kernel, with_tm=with_tm, tn=tn),
        grid=(n // tm, dout // tn),
        in_specs=[pl.BlockSpec((tm, d), lambda i, j: (i, 0)),
                  pl.BlockSpec((1, d), lambda i, j: (0, 0)),
                  pl.BlockSpec((d, tn), lambda i, j: (0, j))],
        out_specs=out_specs,
        out_shape=out_shape,
        scratch_shapes=[pltpu.VMEM((tm, d), BF16)],
        compiler_params=_cparams(("parallel", "arbitrary")),
    )(x2d, g.reshape(1, d), w)
    return res if with_tm else res[0]


def _s5_kernel(u_ref, h0r_ref, h0i_ref, lr_ref, li_ref, ldt_ref, br_ref, bi_ref, cr_ref, ci_ref, d_ref,
               y_ref, hr_out, hi_out, bur, bui, hre, him, par, st_r, st_i, *, nb, tc):
    c = pl.program_id(0)

    @pl.when(c == 0)
    def _():
        lr = lr_ref[...]
        li = li_ref[...]
        dt = jnp.exp(ldt_ref[...])
        mag = jnp.exp(lr * dt)
        abr = mag * jnp.cos(li * dt)
        abi = mag * jnp.sin(li * dt)
        den = lr * lr + li * li
        nr = abr - 1.0
        par[0:1, :] = abr
        par[1:2, :] = abi
        par[2:3, :] = (nr * lr + abi * li) / den
        par[3:4, :] = (abi * lr - nr * li) / den
        st_r[...] = h0r_ref[...]
        st_i[...] = h0i_ref[...]

    u = u_ref[...]
    ub = u.astype(BF16)
    rr = _dot(ub, br_ref[...])
    ri = _dot(ub, bi_ref[...])
    fr = par[2:3, :]
    fi = par[3:4, :]
    bur[...] = fr * rr - fi * ri
    bui[...] = fr * ri + fi * rr
    ar = jnp.broadcast_to(par[0:1, :], (8, S5_W))
    ai = jnp.broadcast_to(par[1:2, :], (8, S5_W))

    for gi in range(nb // 8):
        def body(t, carry, gi=gi):
            hr, hi = carry
            row = pl.multiple_of(t * nb + gi * 8, 8)
            nhr = ar * hr - ai * hi + bur[pl.ds(row, 8), :]
            nhi = ar * hi + ai * hr + bui[pl.ds(row, 8), :]
            hre[pl.ds(row, 8), :] = nhr
            him[pl.ds(row, 8), :] = nhi
            return nhr, nhi

        hr, hi = lax.fori_loop(0, tc, body, (st_r[gi * 8:(gi + 1) * 8, :], st_i[gi * 8:(gi + 1) * 8, :]))
        st_r[gi * 8:(gi + 1) * 8, :] = hr
        st_i[gi * 8:(gi + 1) * 8, :] = hi

    y = _dot(hre[...].astype(BF16), cr_ref[...]) - _dot(him[...].astype(BF16), ci_ref[...]) + d_ref[...] * u
    y_ref[...] = y

    @pl.when(c == pl.num_programs(0) - 1)
    def _():
        hr_out[...] = st_r[...]
        hi_out[...] = st_i[...]


def _s5(u_tm, cb, h0r, h0i, h0_blk, sp, *, nb, tc):
    rows = u_tm.shape[0]
    t_len = rows // nb
    r = tc * nb
    vec = lambda: pl.BlockSpec((1, S5_W), lambda c: (0, 0))
    outs = pl.pallas_call(
        functools.partial(_s5_kernel, nb=nb, tc=tc),
        grid=(t_len // tc,),
        in_specs=[pl.BlockSpec((r, BR_W), lambda c: (c, cb)),
                  pl.BlockSpec((nb, S5_W), lambda c: (h0_blk, 0)),
                  pl.BlockSpec((nb, S5_W), lambda c: (h0_blk, 0)),
                  vec(), vec(), vec(),
                  pl.BlockSpec((BR_W, S5_W), lambda c: (0, 0)),
                  pl.BlockSpec((BR_W, S5_W), lambda c: (0, 0)),
                  pl.BlockSpec((S5_W, BR_W), lambda c: (0, 0)),
                  pl.BlockSpec((S5_W, BR_W), lambda c: (0, 0)),
                  pl.BlockSpec((1, BR_W), lambda c: (0, 0))],
        out_specs=[pl.BlockSpec((r, BR_W), lambda c: (c, 0)),
                   pl.BlockSpec((nb, S5_W), lambda c: (0, 0)),
                   pl.BlockSpec((nb, S5_W), lambda c: (0, 0))],
        out_shape=[jax.ShapeDtypeStruct((rows, BR_W), F32),
                   jax.ShapeDtypeStruct((nb, S5_W), F32),
                   jax.ShapeDtypeStruct((nb, S5_W), F32)],
        scratch_shapes=[pltpu.VMEM((r, S5_W), F32), pltpu.VMEM((r, S5_W), F32),
                        pltpu.VMEM((r, S5_W), F32), pltpu.VMEM((r, S5_W), F32),
                        pltpu.VMEM((8, S5_W), F32),
                        pltpu.VMEM((nb, S5_W), F32), pltpu.VMEM((nb, S5_W), F32)],
        compiler_params=_cparams(("arbitrary",)),
    )(u_tm, h0r, h0i, sp["lr"], sp["li"], sp["ldt"], sp["wbr"], sp["wbi"], sp["wcr"], sp["wci"], sp["d"])
    return outs


def _s5_params(lam_re, lam_im, log_dt, b_re, b_im, c_re, c_im, d):
    eye = jnp.eye(S5_GROUPS, dtype=F32)
    wb = lambda b: jnp.einsum("gpc,gh->gchp", b, eye).reshape(BR_W, S5_W).astype(BF16)
    wc = lambda c: jnp.einsum("gcp,gh->gphc", c, eye).reshape(S5_W, BR_W).astype(BF16)
    return {
        "lr": lam_re.reshape(1, S5_W), "li": lam_im.reshape(1, S5_W),
        "ldt": jnp.repeat(log_dt, S5_STATE).reshape(1, S5_W),
        "wbr": wb(b_re), "wbi": wb(b_im), "wcr": wc(c_re), "wci": wc(c_im),
        "d": d.reshape(1, BR_W),
    }


def _rope_tables(pos, vec_dim):
    r = vec_dim // ROT_FRAC
    half = r // 2
    inv = ROPE_THETA ** (-jnp.arange(half, dtype=F32) / half)
    ang = pos.astype(F32)[:, None] * inv[None, :]
    cos = jnp.cos(ang)
    sin = jnp.sin(ang)
    t = pos.shape[0]
    ones = jnp.ones((t, vec_dim - r), F32)
    zeros = jnp.zeros((t, vec_dim - r), F32)
    zh = jnp.zeros((t, half), F32)
    reps = BR_W // vec_dim
    c = jnp.tile(jnp.concatenate([cos, cos, ones], axis=1), (1, reps))
    s1 = jnp.tile(jnp.concatenate([-sin, zh, zeros], axis=1), (1, reps))
    s2 = jnp.tile(jnp.concatenate([zh, sin, zeros], axis=1), (1, reps))
    return c, s1, s2


def _rope(x, c, s1, s2, half):
    return x * c + pltpu.roll(x, BR_W - half, 1) * s1 + pltpu.roll(x, half, 1) * s2


def _lam_from(lv_ref, lam_init):
    lv = lv_ref[...]
    a = jnp.sum(lv[0:1, :] * lv[1:2, :], axis=-1, keepdims=True)
    b = jnp.sum(lv[2:3, :] * lv[3:4, :], axis=-1, keepdims=True)
    return jnp.exp(a) - jnp.exp(b) + lam_init


def _split(x, n):
    parts = []
    for _ in range(n):
        p = x.astype(BF16)
        parts.append(p)
        x = x - p.astype(F32)
    return parts


def _dot_lsplit(a, b_exact, n):
    b = b_exact.astype(BF16)
    out = None
    for p in _split(a, n):
        t = _dot(p, b)
        out = t if out is None else out + t
    return out


def _dot_nt_rsplit(a_exact, b, n):
    a = a_exact.astype(BF16)
    out = None
    for p in _split(b, n):
        t = _dot_nt(a, p)
        out = t if out is None else out + t
    return out


def _dot_nt3(a, b):
    ah, al = _split(a, 2)
    bh, bl = _split(b, 2)
    return _dot_nt(ah, bh) + _dot_nt(ah, bl) + _dot_nt(al, bh)


def _head_rms(o, gsum_ref, gain):
    ms = _dot_lsplit(o * o, gsum_ref[...], 2)
    return o * lax.rsqrt(ms + EPS) * gain


def _softmax_parts(parts):
    m = parts[0].max(axis=-1, keepdims=True)
    for p in parts[1:]:
        m = jnp.maximum(m, p.max(axis=-1, keepdims=True))
    es = [jnp.exp(p - m) for p in parts]
    l = es[0].sum(axis=-1, keepdims=True)
    for e in es[1:]:
        l = l + e.sum(axis=-1, keepdims=True)
    return es, l


def _moba_select_t(gs_t, n_past):
    rows = lax.broadcasted_iota(jnp.int32, gs_t.shape, 0)
    past = rows < n_past
    sel = jnp.zeros(gs_t.shape, F32)
    for n in range(gs_t.shape[0]):
        gn = gs_t[n:n + 1, :]
        beats = (gs_t > gn) | ((gs_t == gn) & (rows < n))
        cnt = jnp.sum(jnp.where(past & beats, 1.0, 0.0), axis=0, keepdims=True)
        sel = jnp.where((rows == n) & past, jnp.where(cnt < MOBA_TOPK, 1.0, 0.0), sel)
    return sel


def _attn_prompt_kernel(*refs, t_len, mode, lam_init):
    if mode == "moba":
        (q_ref, k_ref, v_ref, c_ref, s1_ref, s2_ref, y_ref, ko_ref, vo_ref, kb, vb, km, m_s, l_s, acc_s) = refs
        half, group, scale = 8, HEAD_DIM, HEAD_DIM ** -0.5
    else:
        (q_ref, k_ref, v_ref, c_ref, s1_ref, s2_ref, lv_ref, gain_ref, gsum_ref,
         y_ref, ko_ref, vo_ref, kb, vb, m_s, l_s, acc_s) = refs
        half, group, scale = 4, DIFF_DK, DIFF_DK ** -0.5
        lam = _lam_from(lv_ref, lam_init)
    qb = MOBA_BLOCK
    nblk = t_len // qb
    per_head = HEAD_DIM // group
    i = pl.program_id(1)
    h = pl.program_id(2)
    lane = lax.broadcasted_iota(jnp.int32, (1, BR_W), 1)
    ri = lax.broadcasted_iota(jnp.int32, (qb, qb), 0)
    ci = lax.broadcasted_iota(jnp.int32, (qb, qb), 1)
    tril = ci <= ri

    @pl.when((i == 0) & (h == 0))
    def _():
        if mode == "moba":
            km[...] = jnp.zeros((8, BR_W), F32)
        for j in range(nblk):
            rows = slice(j * qb, (j + 1) * qb)
            k = _rope(k_ref[rows, :], c_ref[rows, :], s1_ref[rows, :], s2_ref[rows, :], half)
            v = v_ref[rows, :]
            ko_ref[rows, :] = k
            vo_ref[rows, :] = v
            kb[rows, :] = k.astype(BF16)
            vb[rows, :] = v.astype(BF16)
            if mode == "moba":
                km[j:j + 1, :] = jnp.mean(k, axis=0, keepdims=True)

    qrows = pl.ds(pl.multiple_of(i * qb, qb), qb)
    q = _rope(q_ref[...], c_ref[qrows, :], s1_ref[qrows, :], s2_ref[qrows, :], half)
    outs = []
    for g in range(per_head):
        gm = (lane // group) == (h * per_head + g)
        qmb = jnp.where(gm, q, 0.0).astype(BF16)
        s = jnp.where(tril, _dot_nt(qmb, kb[qrows, :]) * scale, NEG)
        m0 = s.max(axis=-1, keepdims=True)
        p = jnp.exp(s - m0)
        m_s[g] = m0
        l_s[g] = p.sum(axis=-1, keepdims=True)
        acc_s[g] = _dot(p.astype(BF16), vb[qrows, :])
        if mode == "moba":
            gs_t = _dot_nt3(jnp.where(gm, km[...], 0.0), q)
            sel_t = _moba_select_t(gs_t, i)
            eye = jnp.where(ri == ci, 1.0, 0.0).astype(BF16)
            sel = _dot_nt(eye, sel_t.astype(BF16))

        def body(j, carry, g=g, qmb=qmb):
            rows = pl.ds(pl.multiple_of(j * qb, qb), qb)
            s = _dot_nt(qmb, kb[rows, :]) * scale
            if mode == "moba":
                onehot = jnp.where(lax.broadcasted_iota(jnp.int32, (8, qb), 0) == j, 1.0, 0.0)
                s = jnp.where(_dot(sel, onehot.astype(F32)) > 0.5, s, NEG)
            m_old = m_s[g]
            m_new = jnp.maximum(m_old, s.max(axis=-1, keepdims=True))
            alpha = jnp.exp(m_old - m_new)
            p = jnp.exp(s - m_new)
            l_s[g] = alpha * l_s[g] + p.sum(axis=-1, keepdims=True)
            acc_s[g] = alpha * acc_s[g] + _dot(p.astype(BF16), vb[rows, :])
            m_s[g] = m_new
            return carry

        lax.fori_loop(0, i, body, 0)
        outs.append(acc_s[g] / l_s[g])
    o_h = outs[0] if mode == "moba" else outs[0] - lam * outs[1]
    o_h = jnp.where((lane // HEAD_DIM) == h, o_h, 0.0)

    @pl.when(h == 0)
    def _():
        y_ref[...] = o_h

    @pl.when(h > 0)
    def _():
        y_ref[...] += o_h

    if mode == "diff":
        @pl.when(h == N_HEADS - 1)
        def _():
            y_ref[...] = _head_rms(y_ref[...], gsum_ref, gain_ref[...]) * (1.0 - lam_init)


def _seq_spec(t_len, cb, ncol, time_major):
    if time_major:
        return pl.BlockSpec((t_len, BR_W), lambda b, *_: (0, b * ncol + cb))
    return pl.BlockSpec((t_len, BR_W), lambda b, *_: (b, cb))


def _full_spec(shape):
    return pl.BlockSpec(shape, lambda b, *_: (0,) * len(shape))


def _gsum():
    h = jnp.arange(BR_W) // HEAD_DIM
    return jnp.where(h[:, None] == h[None, :], 1.0 / HEAD_DIM, 0.0).astype(F32)


def _attn_prompt(p2d, nb, t_len, cbs, tables, mode, lam_init=0.0, lv=None, gain=None):
    n = nb * t_len
    qb = MOBA_BLOCK
    nblk = t_len // qb
    assert nblk <= 8
    seq = lambda cb: _seq_spec(t_len, cb, NCB, False)
    tab = lambda: _full_spec((t_len, BR_W))
    in_specs = [pl.BlockSpec((qb, BR_W), lambda b, i, h: (b * nblk + i, cbs[0])),
                seq(cbs[1]), seq(cbs[2]), tab(), tab(), tab()]
    args = [p2d, p2d, p2d, *tables]
    scratch = [pltpu.VMEM((t_len, BR_W), BF16), pltpu.VMEM((t_len, BR_W), BF16)]
    if mode == "moba":
        scratch.append(pltpu.VMEM((8, BR_W), F32))
        n_maps = 1
    else:
        in_specs += [_full_spec((4, DIFF_DK)), _full_spec((1, BR_W)), _full_spec((BR_W, BR_W))]
        args += [lv, gain, _gsum()]
        n_maps = 2
    scratch += [pltpu.VMEM((n_maps, qb, 1), F32), pltpu.VMEM((n_maps, qb, 1), F32),
                pltpu.VMEM((n_maps, qb, BR_W), F32)]
    out = pl.BlockSpec((t_len, BR_W), lambda b, i, h: (b, 0))
    return pl.pallas_call(
        functools.partial(_attn_prompt_kernel, t_len=t_len, mode=mode, lam_init=lam_init),
        grid=(nb, nblk, N_HEADS),
        in_specs=in_specs,
        out_specs=[pl.BlockSpec((qb, BR_W), lambda b, i, h: (b * nblk + i, 0)), out, out],
        out_shape=[jax.ShapeDtypeStruct((n, BR_W), F32)] * 3,
        scratch_shapes=scratch,
        compiler_params=_cparams(("parallel", "arbitrary", "arbitrary")),
    )(*args)


def _attn_sample_kernel(*refs, t_len, n_pages, mode, lam_init):
    pt_ref = refs[0]
    del pt_ref
    refs = refs[1:]
    q_ref, k_ref, v_ref, c_ref, s1_ref, s2_ref = refs[:6]
    kp_refs = refs[6:6 + n_pages]
    vp_refs = refs[6 + n_pages:6 + 2 * n_pages]
    rest = refs[6 + 2 * n_pages:]
    if mode == "moba":
        y_ref, ko_ref, vo_ref, kb, vb = rest
        half, group, scale = 8, HEAD_DIM, HEAD_DIM ** -0.5
    else:
        lv_ref, gain_ref, gsum_ref, y_ref, ko_ref, vo_ref, kb, vb = rest
        half, group, scale = 4, DIFF_DK, DIFF_DK ** -0.5
        lam = _lam_from(lv_ref, lam_init)
    past = n_pages * PAGE_SIZE
    hms = _head_masks()
    gms = _head_masks(BR_W, group)
    cc, s1, s2 = c_ref[...], s1_ref[...], s2_ref[...]
    q = _rope(q_ref[...], cc, s1, s2, half)
    k = _rope(k_ref[...], cc, s1, s2, half)
    v = v_ref[...]
    ko_ref[...] = k
    vo_ref[...] = v

    sums = []
    for p in range(n_pages):
        kp = kp_refs[p][...]
        kb[p * PAGE_SIZE:(p + 1) * PAGE_SIZE, :] = kp.astype(BF16)
        vb[p * PAGE_SIZE:(p + 1) * PAGE_SIZE, :] = vp_refs[p][...].astype(BF16)
        if mode == "moba":
            sums.append(jnp.sum(kp, axis=0, keepdims=True))

    qst = jnp.concatenate([jnp.where(gm, q, 0.0) for gm in gms], axis=0)
    nrow = qst.shape[0]
    s_past = _dot_nt(qst.astype(BF16), kb[...]) * scale
    s_new = _dot_nt(qst, k) * scale
    rq = lax.broadcasted_iota(jnp.int32, (nrow, t_len), 0) % t_len
    ck = lax.broadcasted_iota(jnp.int32, (nrow, t_len), 1)
    s_new = jnp.where(ck <= rq, s_new, NEG)

    if mode == "moba":
        ppb = MOBA_BLOCK // PAGE_SIZE
        nblk = past // MOBA_BLOCK
        km = jnp.concatenate(
            [sum(sums[n * ppb:(n + 1) * ppb][1:], sums[n * ppb]) / MOBA_BLOCK for n in range(nblk)], axis=0)
        gs = _dot_nt3(qst, km)
        lanes = lax.broadcasted_iota(jnp.int32, gs.shape, 1)
        cnt = jnp.zeros(gs.shape, F32)
        for m in range(nblk):
            gm_ = gs[:, m:m + 1]
            cnt = cnt + jnp.where((gm_ > gs) | ((gm_ == gs) & (lanes > m)), 1.0, 0.0)
        sel = jnp.where(cnt < MOBA_TOPK, 1.0, 0.0)
        blk = lax.broadcasted_iota(jnp.int32, (nblk, past), 1) // MOBA_BLOCK
        expand = jnp.where(blk == lax.broadcasted_iota(jnp.int32, (nblk, past), 0), 1.0, 0.0)
        s_past = jnp.where(_dot(sel, expand.astype(F32)) > 0.5, s_past, NEG)

    es, l = _softmax_parts([s_past, s_new])
    o = (_dot(es[0].astype(BF16), vb[...]) + _dot(es[1], v)) / l
    acc = jnp.zeros((t_len, BR_W), F32)
    per_head = HEAD_DIM // group
    for h in range(N_HEADS):
        r0 = h * per_head * t_len
        if mode == "moba":
            o_h = o[r0:r0 + t_len, :]
        else:
            o_h = o[r0:r0 + t_len, :] - lam * o[r0 + t_len:r0 + 2 * t_len, :]
        acc = acc + jnp.where(hms[h], o_h, 0.0)
    if mode == "diff":
        acc = _head_rms(acc, gsum_ref, gain_ref[...]) * (1.0 - lam_init)
    y_ref[...] = acc


def _attn_sample(p_tm, nb, t_len, cbs, tables, page_table, pool_k, pool_v, pool_off, mode,
                 lam_init=0.0, lv=None, gain=None):
    n_pages = page_table.shape[1]
    seq = lambda cb: _seq_spec(t_len, cb, NCB, True)
    tab = lambda: _full_spec((t_len, BR_W))

    def page_spec(p):
        return pl.BlockSpec((PAGE_SIZE, BR_W), lambda b, pt: (pool_off + pt[b, p], 0))

    in_specs = [seq(cbs[0]), seq(cbs[1]), seq(cbs[2]), tab(), tab(), tab()]
    in_specs += [page_spec(p) for p in range(n_pages)] * 2
    args = [p_tm, p_tm, p_tm, *tables] + [pool_k] * n_pages + [pool_v] * n_pages
    if mode == "diff":
        in_specs += [_full_spec((4, DIFF_DK)), _full_spec((1, BR_W)), _full_spec((BR_W, BR_W))]
        args += [lv, gain, _gsum()]
    out = pl.BlockSpec((t_len, BR_W), lambda b, pt: (0, b))
    past = n_pages * PAGE_SIZE
    return pl.pallas_call(
        functools.partial(_attn_sample_kernel, t_len=t_len, n_pages=n_pages, mode=mode, lam_init=lam_init),
        grid_spec=pltpu.PrefetchScalarGridSpec(
            num_scalar_prefetch=1,
            grid=(nb,),
            in_specs=in_specs,
            out_specs=[out, out, out],
            scratch_shapes=[pltpu.VMEM((past, BR_W), BF16), pltpu.VMEM((past, BR_W), BF16)],
        ),
        out_shape=[jax.ShapeDtypeStruct((t_len, nb * BR_W), F32)] * 3,
        compiler_params=_cparams(("parallel",)),
    )(page_table, *args)


def _mem_attn_kernel(q_ref, mk_ref, mv_ref, y_ref):
    hms = _head_masks()
    mk = mk_ref[...].astype(BF16)
    mv = mv_ref[...].astype(BF16)
    q = q_ref[...]
    acc = jnp.zeros(q.shape, F32)
    for h in range(N_HEADS):
        s = _dot_nt(jnp.where(hms[h], q, 0.0).astype(BF16), mk) * HEAD_DIM ** -0.5
        es, l = _softmax_parts([s])
        acc = acc + jnp.where(hms[h], _dot(es[0].astype(BF16), mv) / l, 0.0)
    y_ref[...] = acc


def _mem_attn(p2d, nb, t_len, time_major, mem_k, mem_v, k_spec, v_spec):
    qb = math.gcd(t_len, 256)
    nq = t_len // qb
    if time_major:
        assert nq == 1
        q_spec = pl.BlockSpec((qb, BR_W), lambda b, i: (0, b * NCB + CB_XQ))
        out_shape = (t_len, nb * BR_W)
        out_spec = pl.BlockSpec((qb, BR_W), lambda b, i: (0, b))
    else:
        q_spec = pl.BlockSpec((qb, BR_W), lambda b, i: (b * nq + i, CB_XQ))
        out_shape = (nb * t_len, BR_W)
        out_spec = pl.BlockSpec((qb, BR_W), lambda b, i: (b * nq + i, 0))
    return pl.pallas_call(
        _mem_attn_kernel,
        grid=(nb, nq),
        in_specs=[q_spec, k_spec, v_spec],
        out_specs=out_spec,
        out_shape=jax.ShapeDtypeStruct(out_shape, F32),
        compiler_params=_cparams(("parallel", "arbitrary")),
    )(p2d, mem_k, mem_v)


def _gdn_kernel(xq_ref, xk_ref, xv_ref, sm_ref, prev_ref, s0_ref, w_ref, alog_ref, dtb_ref,
                eb_ref, eg_ref, gsum_ref, gone_ref, gain_ref,
                y_ref, conv_ref, sfin_ref,
                xs, qs, ks, vs, bs, gs, ys, sbd, *, t_len, t_pad):
    ch = GDN_CHUNK
    hms = _head_masks()
    x_refs = (xq_ref, xk_ref, xv_ref)
    dst = (qs, ks, vs)

    for part in range(3):
        lanes = slice(part * BR_W, (part + 1) * BR_W)
        xs[5:8, lanes] = prev_ref[:, lanes]
        xs[8:8 + t_len, lanes] = x_refs[part][...]
        conv_ref[:, lanes] = xs[8 + t_len - 3:8 + t_len, lanes]
    rb = min(t_len, 256)
    for r0 in range(0, t_len, rb):
        for part in range(3):
            lanes = slice(part * BR_W, (part + 1) * BR_W)
            out = w_ref[0:1, lanes] * xs[5 + r0:5 + r0 + rb, lanes]
            for j in range(1, GDN_CONV):
                out = out + w_ref[j:j + 1, lanes] * xs[5 + j + r0:5 + j + r0 + rb, lanes]
            out = _silu(out)
            if part < 2:
                ss = _dot_lsplit(out * out, gone_ref[...], 2)
                out = out * lax.rsqrt(ss + EPS)
                if part == 0:
                    out = out * HEAD_DIM ** -0.5
            dst[part][r0:r0 + rb, :] = out
        sm = sm_ref[r0:r0 + rb, :]
        bs[r0:r0 + rb, :] = _dot_lsplit(_sigmoid(sm), eb_ref[...], 3)
        z = sm + dtb_ref[...]
        sp = jnp.maximum(z, 0.0) + jnp.log1p(jnp.exp(-jnp.abs(z)))
        gs[r0:r0 + rb, :] = _dot_lsplit(-jnp.exp(alog_ref[...]) * sp, eg_ref[...], 3)
    if t_pad > t_len:
        zpad = jnp.zeros((t_pad - t_len, BR_W), F32)
        for ref in (qs, ks, vs, bs, gs):
            ref[t_len:t_pad, :] = zpad

    for h in range(N_HEADS):
        pieces = []
        if h > 0:
            pieces.append(jnp.zeros((HEAD_DIM, h * HEAD_DIM), F32))
        pieces.append(s0_ref[h])
        if h < N_HEADS - 1:
            pieces.append(jnp.zeros((HEAD_DIM, (N_HEADS - 1 - h) * HEAD_DIM), F32))
        sbd[h * HEAD_DIM:(h + 1) * HEAD_DIM, :] = jnp.concatenate(pieces, axis=1)

    ri = lax.broadcasted_iota(jnp.int32, (ch, ch), 0)
    ci = lax.broadcasted_iota(jnp.int32, (ch, ch), 1)
    tril = ci <= ri
    strict = ci < ri
    tril_b = jnp.where(tril, 1.0, 0.0).astype(BF16)
    hsel = jnp.where(lax.broadcasted_iota(jnp.int32, (N_HEADS * ch, BR_W), 0) // ch
                     == lax.broadcasted_iota(jnp.int32, (N_HEADS * ch, BR_W), 1) // HEAD_DIM,
                     1.0 / HEAD_DIM, 0.0).astype(F32)
    rr = lax.broadcasted_iota(jnp.int32, (BR_W, BR_W), 0) // HEAD_DIM
    rc = lax.broadcasted_iota(jnp.int32, (BR_W, BR_W), 1) // HEAD_DIM
    bdmask = rr == rc
    n_fac = int(math.log2(ch)) - 1

    def chunk(ci_, carry):
        r0 = pl.multiple_of(ci_ * ch, ch)
        rows = pl.ds(r0, ch)
        qc, kc, vc, bc, gc = qs[rows, :], ks[rows, :], vs[rows, :], bs[rows, :], gs[rows, :]
        gc_parts = _split(gc, 3)
        gcum = sum(_dot(tril_b, p) for p in gc_parts[1:]) + _dot(tril_b, gc_parts[0])
        eg = jnp.exp(gcum)
        kbeta = kc * bc
        vbeta = vc * bc
        kbg = kbeta * eg
        kcb = kc.astype(BF16)
        vbb = vbeta.astype(BF16)
        kbgb = kbg.astype(BF16)
        grow = _dot_nt_rsplit(hsel, gcum, 3)
        u = vbeta
        w = kbg
        qks = []
        for h in range(N_HEADS):
            hm = hms[h]
            gd = gcum[:, h * HEAD_DIM:h * HEAD_DIM + 1] - grow[h * ch:(h + 1) * ch, :]
            decay = jnp.where(tril, jnp.exp(jnp.where(tril, gd, 0.0)), 0.0)
            lmat = jnp.where(strict, _dot_nt(jnp.where(hm, kbeta, 0.0).astype(BF16), kcb) * decay, 0.0)
            qks.append(jnp.where(tril, _dot_nt(jnp.where(hm, qc, 0.0).astype(BF16), kcb) * decay, 0.0))
            xo = -lmat
            m = lmat
            for _ in range(n_fac):
                mb = m.astype(BF16)
                m = _dot(mb, mb)
                xo = xo + m + _dot(xo.astype(BF16), m.astype(BF16))
            xob = xo.astype(BF16)
            u = u + jnp.where(hm, _dot(xob, vbb), 0.0)
            w = w + jnp.where(hm, _dot(xob, kbgb), 0.0)
        s = sbd[...]
        vnew = u - _dot(w, s)
        o = _dot(qc * eg, s)
        for h in range(N_HEADS):
            o = o + jnp.where(hms[h], _dot(qks[h], vnew), 0.0)
        glast = gcum[ch - 1:ch, :]
        kg = kc * jnp.exp(glast - gcum)
        sbd[...] = s * jnp.exp(glast) + jnp.where(bdmask, _dot_tn(kg, vnew), 0.0)
        ys[rows, :] = _head_rms(o, gsum_ref, gain_ref[...])
        return carry

    lax.fori_loop(0, t_pad // ch, chunk, 0)
    y_ref[...] = ys[0:t_len, :]
    s = sbd[...]
    for h in range(N_HEADS):
        sfin_ref[h] = s[h * HEAD_DIM:(h + 1) * HEAD_DIM, h * HEAD_DIM:(h + 1) * HEAD_DIM]


def _gdn(p2d, nb, t_len, time_major, prev, prev_off, s0, s0_off, gp):
    t_pad = -(-t_len // GDN_CHUNK) * GDN_CHUNK
    seq = lambda cb: _seq_spec(t_len, cb, NCB, time_major)
    if time_major:
        sm_spec = pl.BlockSpec((t_len, 128), lambda b: (0, b * (D_PAD // 128) + CB128_SMALL))
        out_shape = (t_len, nb * BR_W)
        out_spec = pl.BlockSpec((t_len, BR_W), lambda b: (0, b))
    else:
        sm_spec = pl.BlockSpec((t_len, 128), lambda b: (b, CB128_SMALL))
        out_shape = (nb * t_len, BR_W)
        out_spec = pl.BlockSpec((t_len, BR_W), lambda b: (b, 0))
    w3 = 3 * BR_W
    return pl.pallas_call(
        functools.partial(_gdn_kernel, t_len=t_len, t_pad=t_pad),
        grid=(nb,),
        in_specs=[seq(CB_DQ), seq(CB_DK), seq(CB_DV), sm_spec,
                  pl.BlockSpec((None, GDN_CONV - 1, w3), lambda b: (prev_off + b, 0, 0)),
                  pl.BlockSpec((None, N_HEADS, HEAD_DIM, HEAD_DIM), lambda b: (s0_off + b, 0, 0, 0)),
                  _full_spec((GDN_CONV, w3)), _full_spec((1, 128)), _full_spec((1, 128)),
                  _full_spec((128, BR_W)), _full_spec((128, BR_W)),
                  _full_spec((BR_W, BR_W)), _full_spec((BR_W, BR_W)), _full_spec((1, BR_W))],
        out_specs=[out_spec,
                   pl.BlockSpec((None, GDN_CONV - 1, w3), lambda b: (b, 0, 0)),
                   pl.BlockSpec((None, N_HEADS, HEAD_DIM, HEAD_DIM), lambda b: (b, 0, 0, 0))],
        out_shape=[jax.ShapeDtypeStruct(out_shape, F32),
                   jax.ShapeDtypeStruct((nb, GDN_CONV - 1, w3), F32),
                   jax.ShapeDtypeStruct((nb, N_HEADS, HEAD_DIM, HEAD_DIM), F32)],
        scratch_shapes=[pltpu.VMEM((t_len + 8, w3), F32)] + [pltpu.VMEM((t_pad, BR_W), F32)] * 6
                       + [pltpu.VMEM((BR_W, BR_W), F32)],
        compiler_params=_cparams(("parallel",)),
    )(p2d, p2d, p2d, p2d, prev, s0, gp["w"], gp["alog"], gp["dtb"], gp["eb"], gp["eg"],
      _gsum(), gp["gone"], gp["gain"])


def _gdn_params(conv_w, a_log, dt_bias, norm_g):
    lane = jnp.arange(128)
    head = jnp.arange(BR_W) // HEAD_DIM
    place = lambda v: jnp.zeros((1, 128), F32).at[0, N_HEADS:2 * N_HEADS].set(v)
    return {
        "w": conv_w, "alog": place(a_log), "dtb": place(dt_bias),
        "eb": jnp.where(lane[:, None] == head[None, :], 1.0, 0.0).astype(F32),
        "eg": jnp.where(lane[:, None] == head[None, :] + N_HEADS, 1.0, 0.0).astype(F32),
        "gone": jnp.where(head[:, None] == head[None, :], 1.0, 0.0).astype(F32),
        "gain": jnp.tile(norm_g, N_HEADS).reshape(1, BR_W),
    }


def _merge_kernel(x_ref, ya_ref, yb_ref, yc_ref, yd_ref, yx_ref, az_ref, bz_ref, cz_ref, dz_ref, gt_ref,
                  wglu_ref, wbr_ref, wout_ref, g_ref, o_ref):
    ya = jax.nn.gelu(ya_ref[...])
    glu = _dot(ya.astype(BF16), wglu_ref[...])
    ya = glu[:, :BR_W] * _sigmoid(glu[:, BR_W:]) * _silu(az_ref[...])
    ys = [ya,
          yb_ref[...] * _silu(bz_ref[...]),
          yc_ref[...] * _silu(cz_ref[...]),
          yd_ref[...] * _silu(dz_ref[...]),
          yx_ref[...]]
    merged = None
    for n in range(N_BRANCH):
        zb = _dot(ys[n].astype(BF16), wbr_ref[n])
        term = _sigmoid(gt_ref[:, n * D_MODEL:(n + 1) * D_MODEL]) * zb
        merged = term if merged is None else merged + term
    out = _dot(merged.astype(BF16), wout_ref[...])
    y = out * lax.rsqrt(jnp.mean(out * out, axis=-1, keepdims=True) + EPS) * g_ref[...]
    o_ref[...] = x_ref[...] + y


def _merge(x2d, p2d, ya, yb, yc, yd, yx, ya_tm_nb, wglu, wbr, wout, g, *, tm, seq_len):
    n = x2d.shape[0]
    row = lambda width: pl.BlockSpec((tm, width), lambda i: (i, 0))
    pcol = lambda cb: pl.BlockSpec((tm, BR_W), lambda i: (i, cb))
    if ya_tm_nb:
        tpb = seq_len // tm
        ya_spec = pl.BlockSpec((tm, BR_W), lambda i: (i % tpb, i // tpb))
    else:
        ya_spec = row(BR_W)
    return pl.pallas_call(
        _merge_kernel,
        grid=(n // tm,),
        in_specs=[row(D_MODEL), ya_spec, row(BR_W), row(BR_W), row(BR_W), row(BR_W),
                  pcol(CB_AZ), pcol(CB_BZ), pcol(CB_CZ), pcol(CB_DZ),
                  pl.BlockSpec((tm, N_BRANCH * D_MODEL), lambda i: (i, 0)),
                  pl.BlockSpec((BR_W, 2 * BR_W), lambda i: (0, 0)),
                  pl.BlockSpec((N_BRANCH, BR_W, D_MODEL), lambda i: (0, 0, 0)),
                  pl.BlockSpec((D_MODEL, D_MODEL), lambda i: (0, 0)),
                  pl.BlockSpec((1, D_MODEL), lambda i: (0, 0))],
        out_specs=row(D_MODEL),
        out_shape=jax.ShapeDtypeStruct((n, D_MODEL), F32),
        compiler_params=_cparams(("parallel",)),
    )(x2d, ya, yb, yc, yd, yx, p2d, p2d, p2d, p2d, p2d, wglu, wbr, wout, g.reshape(1, D_MODEL))


def _prep_w_in(w):
    d = w.shape[0]
    lo = 13 * BR_W
    hi = lo + 2 * N_HEADS
    gates = hi + 2 * BR_W
    pad = jnp.zeros((d, D_PAD - w.shape[1]), w.dtype)
    return jnp.concatenate([w[:, gates:], w[:, :lo], w[:, hi:gates], w[:, lo:hi], pad], axis=1).astype(BF16)


def kernel(x_prompt, x_sample, state_s5_re, state_s5_im, cache_moba_k, cache_moba_v, cache_diff_k, cache_diff_v, state_gdn_conv, state_gdn, cache_mem_k, cache_mem_v, page_table, mem_prompt, norm_pre, norm_post, norm_mem, w_in, s5_lam_re, s5_lam_im, s5_log_dt, s5_b_re, s5_b_im, s5_c_re, s5_c_im, s5_d, w_glu, diff_lam, diff_norm, gdn_conv_w, gdn_a_log, gdn_dt_bias, gdn_norm, w_mem_kv, w_branch, w_out):
    depth = w_in.shape[0]
    bp, tp, d = x_prompt.shape
    bs, ts, _ = x_sample.shape
    n_phys = cache_moba_k.shape[1]
    past_len = page_table.shape[1] * PAGE_SIZE
    w3 = 3 * BR_W

    xp = x_prompt.reshape(bp * tp, d)
    xs = jnp.transpose(x_sample, (1, 0, 2)).reshape(ts * bs, d)
    mem2d = mem_prompt.reshape(bp * MEM_LEN, d)

    pos_p = jnp.arange(tp)
    pos_s = past_len + jnp.arange(ts)
    tab_p = {64: _rope_tables(pos_p, HEAD_DIM), 32: _rope_tables(pos_p, DIFF_DK)}
    tab_s = {64: _rope_tables(pos_s, HEAD_DIM), 32: _rope_tables(pos_s, DIFF_DK)}

    pool = lambda c: c.reshape(depth * n_phys * PAGE_SIZE, BR_W)
    pool_mk, pool_mv, pool_dk, pool_dv = pool(cache_moba_k), pool(cache_moba_v), pool(cache_diff_k), pool(cache_diff_v)
    s5r_all = state_s5_re.reshape(depth * bs, S5_W)
    s5i_all = state_s5_im.reshape(depth * bs, S5_W)
    conv_all = state_gdn_conv.reshape(depth * bs, GDN_CONV - 1, w3)
    gdn_all = state_gdn.reshape(depth * bs, N_HEADS, HEAD_DIM, HEAD_DIM)
    memk_all = cache_mem_k.reshape(depth * bs * MEM_LEN, BR_W)
    memv_all = cache_mem_v.reshape(depth * bs * MEM_LEN, BR_W)
    zero_s5 = jnp.zeros((bp, S5_W), F32)
    zero_conv = jnp.zeros((bp, GDN_CONV - 1, w3), F32)
    zero_gdn = jnp.zeros((bp, N_HEADS, HEAD_DIM, HEAD_DIM), F32)

    p_out = [[] for _ in range(10)]
    s_out = [[] for _ in range(8)]
    tm_tok = lambda y: jnp.transpose(y.reshape(ts, bs, N_HEADS, HEAD_DIM), (1, 0, 2, 3))

    for l in range(depth):
        lam_init = 0.8 - 0.6 * math.exp(-0.3 * l)
        w_p = _prep_w_in(w_in[l])
        sp = _s5_params(s5_lam_re[l], s5_lam_im[l], s5_log_dt[l], s5_b_re[l], s5_b_im[l],
                        s5_c_re[l], s5_c_im[l], s5_d[l])
        gp = _gdn_params(gdn_conv_w[l], gdn_a_log[l], gdn_dt_bias[l], gdn_norm[l])
        dgain = jnp.tile(diff_norm[l], N_HEADS).reshape(1, BR_W)
        wglu = w_glu[l].astype(BF16)
        wbr = w_branch[l].astype(BF16)
        wout = w_out[l].astype(BF16)

        kv = _proj(mem2d, norm_mem[l], w_mem_kv[l].astype(BF16), tm=math.gcd(bp * MEM_LEN, 512), tn=BR_W)
        pp, au_tm = _proj(xp, norm_pre[l], w_p, tm=math.gcd(tp, 1024), tn=1024, seq_len=tp)
        ya, hr, hi = _s5(au_tm.reshape(tp * bp, BR_W), 0, zero_s5, zero_s5, 0, sp, nb=bp, tc=64)
        yb, mk, mv = _attn_prompt(pp, bp, tp, (CB_BQ, CB_BK, CB_BV), tab_p[64], "moba")
        yc, dk, dv = _attn_prompt(pp, bp, tp, (CB_CQ, CB_CK, CB_CV), tab_p[32], "diff",
                                  lam_init, diff_lam[l], dgain)
        yd, conv_new, s_new = _gdn(pp, bp, tp, False, zero_conv, 0, zero_gdn, 0, gp)
        mem_spec = lambda cb: pl.BlockSpec((MEM_LEN, BR_W), lambda b, i: (b, cb))
        yx = _mem_attn(pp, bp, tp, False, kv, kv, mem_spec(0), mem_spec(1))
        xp = _merge(xp, pp, ya.reshape(tp, bp * BR_W), yb, yc, yd, yx, bp, wglu, wbr, wout, norm_post[l],
                    tm=256, seq_len=tp)
        outs = (hr.reshape(bp, S5_GROUPS, S5_STATE), hi.reshape(bp, S5_GROUPS, S5_STATE),
                mk.reshape(bp, tp, N_HEADS, HEAD_DIM), mv.reshape(bp, tp, N_HEADS, HEAD_DIM),
                dk.reshape(bp, tp, N_HEADS, HEAD_DIM), dv.reshape(bp, tp, N_HEADS, HEAD_DIM),
                conv_new, s_new,
                kv[:, :BR_W].reshape(bp, MEM_LEN, N_HEADS, HEAD_DIM),
                kv[:, BR_W:].reshape(bp, MEM_LEN, N_HEADS, HEAD_DIM))
        for lst, a in zip(p_out, outs):
            lst.append(a)

        ps = _proj(xs, norm_pre[l], w_p, tm=math.gcd(ts * bs, 1024), tn=1024)
        ps_v = ps.reshape(ts, bs * D_PAD)
        ya, hr, hi = _s5(ps, CB_AU, s5r_all, s5i_all, l, sp, nb=bs, tc=ts)
        yb, mk, mv = _attn_sample(ps_v, bs, ts, (CB_BQ, CB_BK, CB_BV), tab_s[64], page_table,
                                  pool_mk, pool_mv, l * n_phys, "moba")
        yc, dk, dv = _attn_sample(ps_v, bs, ts, (CB_CQ, CB_CK, CB_CV), tab_s[32], page_table,
                                  pool_dk, pool_dv, l * n_phys, "diff", lam_init, diff_lam[l], dgain)
        yd, conv_new, s_new = _gdn(ps_v, bs, ts, True, conv_all, l * bs, gdn_all, l * bs, gp)
        memc = lambda: pl.BlockSpec((MEM_LEN, BR_W), lambda b, i: (l * bs + b, 0))
        yx = _mem_attn(ps_v, bs, ts, True, memk_all, memv_all, memc(), memc())
        flat = lambda y: y.reshape(ts * bs, BR_W)
        xs = _merge(xs, ps, ya, flat(yb), flat(yc), flat(yd), flat(yx), 0, wglu, wbr, wout, norm_post[l],
                    tm=math.gcd(ts * bs, 256), seq_len=ts)
        outs = (hr.reshape(bs, S5_GROUPS, S5_STATE), hi.reshape(bs, S5_GROUPS, S5_STATE),
                tm_tok(mk), tm_tok(mv), tm_tok(dk), tm_tok(dv), conv_new, s_new)
        for lst, a in zip(s_out, outs):
            lst.append(a)

    y_prompt = xp.reshape(bp, tp, d)
    y_sample = jnp.transpose(xs.reshape(ts, bs, d), (1, 0, 2))
    p_st = [jnp.stack(a) for a in p_out]
    s_st = [jnp.stack(a) for a in s_out]
    return (y_prompt, y_sample, *p_st[:8], p_st[8], p_st[9], *s_st)
```
